```python
import jax, jax.numpy as jnp
from jax import lax
import numpy as np

D_MODEL = 1024
BATCH = 16
SEQ = 2048
DEPTH = 1

CONV_WIDTH = D_MODEL
CONV_KERNEL = 31
GLA_HEADS = 4
GLA_KEY_DIM = D_MODEL // 2
GLA_VAL_DIM = D_MODEL
GLA_DK = GLA_KEY_DIM // GLA_HEADS
GLA_DV = GLA_VAL_DIM // GLA_HEADS
GATE_RANK = 16
GATE_TEMP = 16.0
CHUNK = 64
EPS = 1e-6

IN_SIZES = (CONV_WIDTH, CONV_WIDTH, CONV_WIDTH,
            GLA_KEY_DIM, GLA_KEY_DIM, GLA_VAL_DIM, GATE_RANK, GLA_VAL_DIM,
            D_MODEL, D_MODEL)
IN_SPLITS = tuple(int(s) for s in np.cumsum(IN_SIZES)[:-1])
IN_WIDTH = int(sum(IN_SIZES))

kernel_name = "hybrid_conformer_conv_gla_gated_merge"


def rms_norm(x, g):
    xf = x.astype(jnp.float32)
    y = xf * lax.rsqrt(jnp.mean(xf * xf, axis=-1, keepdims=True) + EPS)
    return (y * g.astype(jnp.float32)).astype(x.dtype)


def layer_norm(x, g, b):
    xf = x.astype(jnp.float32)
    mu = jnp.mean(xf, axis=-1, keepdims=True)
    xc = xf - mu
    y = xc * lax.rsqrt(jnp.mean(xc * xc, axis=-1, keepdims=True) + EPS)
    return (y * g.astype(jnp.float32) + b.astype(jnp.float32)).astype(x.dtype)


def conformer_conv_branch(u_val, u_gate, z, conv_w, conv_b, ln_g, ln_b, w_proj):
    u = u_val * jax.nn.sigmoid(u_gate)
    u = lax.conv_general_dilated(
        u, conv_w[:, None, :].astype(u.dtype),
        window_strides=(1,), padding=[(CONV_KERNEL - 1, 0)],
        dimension_numbers=('NWC', 'WIO', 'NWC'),
        feature_group_count=CONV_WIDTH) + conv_b
    u = jax.nn.silu(layer_norm(u, ln_g, ln_b))
    u = u * jax.nn.silu(z)
    return u @ w_proj


def gla_chunked(q, k, v, log_a):
    B, S, H, DK = q.shape
    DV = v.shape[-1]
    N = S // CHUNK
    f32 = jnp.float32
    q = (q.astype(f32) * (DK ** -0.5)).reshape(B, N, CHUNK, H, DK)
    k = k.astype(f32).reshape(B, N, CHUNK, H, DK)
    v = v.astype(f32).reshape(B, N, CHUNK, H, DV)
    b = jnp.cumsum(log_a.astype(f32).reshape(B, N, CHUNK, H, DK), axis=2)
    b_last = b[:, :, -1]
    ref = b[:, :, CHUNK // 2 - 1:CHUNK // 2]
    scores = jnp.einsum('bnihd,bnjhd->bnhij', q * jnp.exp(b - ref), k * jnp.exp(ref - b))
    causal = jnp.tril(jnp.ones((CHUNK, CHUNK), dtype=bool))
    scores = jnp.where(causal, scores, 0.0)
    o_intra = jnp.einsum('bnhij,bnjhv->bnihv', scores, v)
    kv = jnp.einsum('bnjhd,bnjhv->bnhdv', k * jnp.exp(b_last[:, :, None] - b), v)
    decay = jnp.exp(b_last)

    def step(state, inp):
        dec, kv_n = inp
        return dec[..., None] * state + kv_n, state

    _, s_prev = lax.scan(step, jnp.zeros((B, H, DK, DV), f32),
                         (jnp.moveaxis(decay, 1, 0), jnp.moveaxis(kv, 1, 0)))
    s_prev = jnp.moveaxis(s_prev, 0, 1)
    o_inter = jnp.einsum('bnihd,bnhdv->bnihv', q * jnp.exp(b), s_prev)
    return (o_intra + o_inter).reshape(B, S, H, DV)


def setup_inputs(seed: int = 0) -> dict:
    key = jax.random.key(seed)
    ks = jax.random.split(key, 16)
    f32 = jnp.float32
    nrm = lambda k, shape, scale: jax.random.normal(k, shape, f32) * scale
    L = DEPTH
    return {
        "x": jax.random.normal(ks[0], (BATCH, SEQ, D_MODEL), f32),
        "norm_g": 1.0 + nrm(ks[1], (L, D_MODEL), 0.02),
        "w_in": nrm(ks[2], (L, D_MODEL, IN_WIDTH), D_MODEL ** -0.5),
        "conv_w": nrm(ks[3], (L, CONV_KERNEL, CONV_WIDTH), CONV_KERNEL ** -0.5),
        "conv_b": nrm(ks[4], (L, CONV_WIDTH), 0.02),
        "conv_ln_g": 1.0 + nrm(ks[5], (L, CONV_WIDTH), 0.02),
        "conv_ln_b": nrm(ks[6], (L, CONV_WIDTH), 0.02),
        "w_conv_out": nrm(ks[7], (L, CONV_WIDTH, D_MODEL), CONV_WIDTH ** -0.5),
        "gate_w2": nrm(ks[8], (L, GATE_RANK, GLA_KEY_DIM), GATE_RANK ** -0.5),
        "gate_b": nrm(ks[9], (L, GLA_KEY_DIM), 0.1),
        "gla_norm_g": 1.0 + nrm(ks[10], (L, GLA_DV), 0.02),
        "w_gla_out": nrm(ks[11], (L, GLA_VAL_DIM, D_MODEL), GLA_VAL_DIM ** -0.5),
        "w_out": nrm(ks[12], (L, D_MODEL, D_MODEL), D_MODEL ** -0.5),
        "final_g": 1.0 + nrm(ks[13], (D_MODEL,), 0.02),
    }


def reference(x, norm_g, w_in, conv_w, conv_b, conv_ln_g, conv_ln_b, w_conv_out,
              gate_w2, gate_b, gla_norm_g, w_gla_out, w_out, final_g):
    B, S, _ = x.shape
    for l in range(DEPTH):
        h = rms_norm(x, norm_g[l])
        proj = h @ w_in[l]
        (c_val, c_gate, c_z, q, k, v, g_lr, g_r, m_conv, m_gla) = jnp.split(proj, IN_SPLITS, axis=-1)

        y_conv = conformer_conv_branch(c_val, c_gate, c_z, conv_w[l], conv_b[l],
                                       conv_ln_g[l], conv_ln_b[l], w_conv_out[l])

        z = (g_lr @ gate_w2[l] + gate_b[l]).astype(jnp.float32)
        log_a = jax.nn.log_sigmoid(z) / GATE_TEMP
        o = gla_chunked(q.reshape(B, S, GLA_HEADS, GLA_DK),
                        k.reshape(B, S, GLA_HEADS, GLA_DK),
                        v.reshape(B, S, GLA_HEADS, GLA_DV),
                        log_a.reshape(B, S, GLA_HEADS, GLA_DK))
        o = rms_norm(o, gla_norm_g[l]).astype(x.dtype).reshape(B, S, GLA_VAL_DIM)
        y_gla = (o * jax.nn.silu(g_r)) @ w_gla_out[l]

        y = jax.nn.sigmoid(m_conv) * y_conv + jax.nn.sigmoid(m_gla) * y_gla
        x = x + y @ w_out[l]
    return rms_norm(x, final_g)
```

```python
import functools

import jax
import jax.numpy as jnp
from jax import lax
from jax.experimental import pallas as pl
from jax.experimental.pallas import tpu as pltpu

D_MODEL = 1024
CONV_KERNEL = 31
GLA_HEADS = 4
GLA_KEY_DIM = D_MODEL // 2
GLA_DK = GLA_KEY_DIM // GLA_HEADS
GLA_DV = D_MODEL // GLA_HEADS
GATE_RANK = 16
GATE_TEMP = 16.0
CHUNK = 64
EPS = 1e-6

LANES = 128
SUBLANES = 8
HALO = 32
SEQ_TILE = 256
CONV_ROWS = 32
CONV_COLS = 512
VMEM_LIMIT_BYTES = 52 * 1024 * 1024

_W_SIZES = (D_MODEL, D_MODEL, D_MODEL, GLA_KEY_DIM, GLA_KEY_DIM, D_MODEL, D_MODEL, D_MODEL, D_MODEL)
_W_OFFS = tuple(sum(_W_SIZES[:i]) for i in range(len(_W_SIZES)))
(_C_VAL, _C_GATE, _C_Z, _Q, _K, _V, _G_R, _M_CONV, _M_GLA) = range(9)

_F32 = jnp.float32
_BF16 = jnp.bfloat16


def _sigmoid(x):
    return 1.0 / (1.0 + jnp.exp(-x))


def _silu(x):
    return x * _sigmoid(x)


def _split3(x):
    p0 = x.astype(_BF16)
    r = x - p0.astype(_F32)
    p1 = r.astype(_BF16)
    p2 = (r - p1.astype(_F32)).astype(_BF16)
    return p0, p1, p2


def _layer_kernel(x_ref, norm_g_ref, w_main_ref, w_glr_ref, conv_w_ref, conv_b_ref,
                  ln_g_ref, ln_b_ref, w_conv_out_ref, gate_w2_ref, gate_b_ref,
                  gla_g_ref, w_gla_out_ref, w_out_ref, final_g_ref,
                  out_ref,
                  h_s, ubuf, conv_s, q_s, k_s, v_s, la_s, o_s, state,
                  *, apply_final_norm):
    ts = x_ref.shape[1]

    @pl.when(pl.program_id(1) == 0)
    def _():
        ubuf[0:HALO, :] = jnp.zeros((HALO, D_MODEL), _F32)
        state[...] = jnp.zeros(state.shape, _F32)

    x = x_ref[0]
    h = x * lax.rsqrt(jnp.mean(x * x, axis=-1, keepdims=True) + EPS) * norm_g_ref[...]
    h_s[...] = h.astype(_BF16)

    def proj(idx):
        off = _W_OFFS[idx]
        return jnp.dot(h_s[...], w_main_ref[:, off:off + _W_SIZES[idx]],
                       preferred_element_type=_F32)

    ubuf[HALO:HALO + ts, :] = proj(_C_VAL) * _sigmoid(proj(_C_GATE))

    for r0 in range(0, ts, CONV_ROWS):
        for c0 in range(0, D_MODEL, CONV_COLS):
            cs = slice(c0, c0 + CONV_COLS)
            acc = jnp.broadcast_to(conv_b_ref[:, cs], (CONV_ROWS, CONV_COLS))
            for k in range(CONV_KERNEL):
                start = r0 + (HALO - (CONV_KERNEL - 1)) + k
                acc = acc + conv_w_ref[k:k + 1, cs] * ubuf[start:start + CONV_ROWS, cs]
            conv_s[r0:r0 + CONV_ROWS, cs] = acc
    ubuf[0:HALO, :] = ubuf[ts:ts + HALO, :]

    c = conv_s[...]
    mu = jnp.mean(c, axis=-1, keepdims=True)
    xc = c - mu
    ln = xc * lax.rsqrt(jnp.mean(xc * xc, axis=-1, keepdims=True) + EPS)
    ln = ln * ln_g_ref[...] + ln_b_ref[...]
    a = _silu(ln) * _silu(proj(_C_Z))
    y_conv = jnp.dot(a.astype(_BF16), w_conv_out_ref[...], preferred_element_type=_F32)

    q_s[...] = proj(_Q) * (GLA_DK ** -0.5)
    k_s[...] = proj(_K)
    v_s[...] = proj(_V).astype(_BF16)
    g_lr = jnp.dot(h_s[...], w_glr_ref[...], preferred_element_type=_F32)
    z = jnp.dot(g_lr.astype(_BF16), gate_w2_ref[...], preferred_element_type=_F32) + gate_b_ref[...]
    la_s[...] = jax.nn.log_sigmoid(z) * (1.0 / GATE_TEMP)

    row = lax.broadcasted_iota(jnp.int32, (CHUNK, CHUNK), 0)
    col = lax.broadcasted_iota(jnp.int32, (CHUNK, CHUNK), 1)
    causal = col <= row
    tril = jnp.where(causal, 1.0, 0.0).astype(_BF16)

    def chunk_body(ci, carry):
        r0 = pl.multiple_of(ci * CHUNK, CHUNK)
        rows = pl.ds(r0, CHUNK)
        b = sum(jnp.dot(tril, p, preferred_element_type=_F32) for p in _split3(la_s[rows, :]))
        for hh in range(GLA_HEADS):
            ks = slice(hh * GLA_DK, (hh + 1) * GLA_DK)
            vs = slice(hh * GLA_DV, (hh + 1) * GLA_DV)
            bh = b[:, ks]
            b_mid = bh[CHUNK // 2 - 1:CHUNK // 2, :]
            b_last = bh[CHUNK - 1:CHUNK, :]
            qh = q_s[rows, ks]
            kh = k_s[rows, ks]
            vh = v_s[rows, vs]
            s_prev = state[hh]
            qe = (qh * jnp.exp(bh - b_mid)).astype(_BF16)
            ke = (kh * jnp.exp(b_mid - bh)).astype(_BF16)
            scores = lax.dot_general(qe, ke, (((1,), (1,)), ((), ())),
                                     preferred_element_type=_F32)
            scores = jnp.where(causal, scores, 0.0)
            o = jnp.dot(scores.astype(_BF16), vh, preferred_element_type=_F32)
            qb = (qh * jnp.exp(bh)).astype(_BF16)
            o = o + jnp.dot(qb, s_prev.astype(_BF16), preferred_element_type=_F32)
            kd = (kh * jnp.exp(b_last - bh)).astype(_BF16)
            kv = lax.dot_general(kd, vh, (((0,), (0,)), ((), ())),
                                 preferred_element_type=_F32)
            decay = jnp.exp(jnp.transpose(jnp.broadcast_to(b_last, (GLA_DK, GLA_DK))))
            state[hh] = jnp.concatenate([decay, decay], axis=1) * s_prev + kv
            o_s[rows, vs] = o
        return carry

    lax.fori_loop(0, ts // CHUNK, chunk_body, 0)

    g_r = _silu(proj(_G_R))
    gated = []
    for hh in range(GLA_HEADS):
        vs = slice(hh * GLA_DV, (hh + 1) * GLA_DV)
        oh = o_s[:, vs]
        oh = oh * lax.rsqrt(jnp.mean(oh * oh, axis=-1, keepdims=True) + EPS) * gla_g_ref[...]
        gated.append((oh * g_r[:, vs]).astype(_BF16))
    y_gla = jnp.dot(jnp.concatenate(gated, axis=1), w_gla_out_ref[...],
                    preferred_element_type=_F32)

    y = _sigmoid(proj(_M_CONV)) * y_conv + _sigmoid(proj(_M_GLA)) * y_gla
    xn = x_ref[0] + jnp.dot(y.astype(_BF16), w_out_ref[...], preferred_element_type=_F32)
    if apply_final_norm:
        xn = xn * lax.rsqrt(jnp.mean(xn * xn, axis=-1, keepdims=True) + EPS) * final_g_ref[...]
    out_ref[0] = xn


def _const_spec(shape):
    zeros = (0,) * len(shape)
    return pl.BlockSpec(shape, lambda b, s: zeros, pipeline_mode=pl.Buffered(1))


def _layer(x, norm_g, w_main, w_glr, conv_w, conv_b, ln_g, ln_b, w_conv_out, gate_w2, gate_b,
           gla_g, w_gla_out, w_out, final_g, *, apply_final_norm):
    batch, seq, d = x.shape
    assert d == D_MODEL and seq % SEQ_TILE == 0 and SEQ_TILE % CHUNK == 0
    assert SEQ_TILE % CONV_ROWS == 0 and HALO >= CONV_KERNEL - 1
    ts = SEQ_TILE
    consts = (norm_g, w_main, w_glr, conv_w, conv_b, ln_g, ln_b, w_conv_out, gate_w2, gate_b,
              gla_g, w_gla_out, w_out, final_g)
    x_spec = pl.BlockSpec((1, ts, d), lambda b, s: (b, s, 0))
    return pl.pallas_call(
        functools.partial(_layer_kernel, apply_final_norm=apply_final_norm),
        out_shape=jax.ShapeDtypeStruct(x.shape, x.dtype),
        grid=(batch, seq // ts),
        in_specs=[x_spec] + [_const_spec(c.shape) for c in consts],
        out_specs=x_spec,
        scratch_shapes=[
            pltpu.VMEM((ts, d), _BF16),
            pltpu.VMEM((HALO + ts, d), _F32),
            pltpu.VMEM((ts, d), _F32),
            pltpu.VMEM((ts, GLA_KEY_DIM), _F32),
            pltpu.VMEM((ts, GLA_KEY_DIM), _F32),
            pltpu.VMEM((ts, d), _BF16),
            pltpu.VMEM((ts, GLA_KEY_DIM), _F32),
            pltpu.VMEM((ts, d), _F32),
            pltpu.VMEM((GLA_HEADS, GLA_DK, GLA_DV), _F32),
        ],
        compiler_params=pltpu.CompilerParams(
            dimension_semantics=("arbitrary", "arbitrary"),
            vmem_limit_bytes=VMEM_LIMIT_BYTES),
        name="conv_gla_layer",
    )(x, *consts)


def kernel(x, norm_g, w_in, conv_w, conv_b, conv_ln_g, conv_ln_b, w_conv_out, gate_w2, gate_b,
           gla_norm_g, w_gla_out, w_out, final_g):
    depth = w_in.shape[0]
    row = lambda v: v.reshape(1, -1)
    lr0 = 3 * D_MODEL + 2 * GLA_KEY_DIM + D_MODEL
    for l in range(depth):
        w = w_in[l]
        w_main = jnp.concatenate([w[:, :lr0], w[:, lr0 + GATE_RANK:]], axis=1).astype(_BF16)
        w_glr = jnp.pad(w[:, lr0:lr0 + GATE_RANK], ((0, 0), (0, LANES - GATE_RANK))).astype(_BF16)
        gate_w2_p = jnp.pad(gate_w2[l], ((0, LANES - GATE_RANK), (0, 0))).astype(_BF16)
        x = _layer(x, row(norm_g[l]), w_main, w_glr, conv_w[l], row(conv_b[l]),
                   row(conv_ln_g[l]), row(conv_ln_b[l]), w_conv_out[l].astype(_BF16),
                   gate_w2_p, row(gate_b[l]), row(gla_norm_g[l]), w_gla_out[l].astype(_BF16),
                   w_out[l].astype(_BF16), row(final_g), apply_final_norm=(l == depth - 1))
    return x
```

```python
import functools

import jax
import jax.numpy as jnp
from jax import lax
from jax.experimental import pallas as pl
from jax.experimental.pallas import tpu as pltpu

D_MODEL = 1024
CONV_KERNEL = 31
GLA_HEADS = 4
GLA_KEY_DIM = D_MODEL // 2
GLA_DK = GLA_KEY_DIM // GLA_HEADS
GLA_DV = D_MODEL // GLA_HEADS
GATE_RANK = 16
GATE_TEMP = 16.0
CHUNK = 64
EPS = 1e-6

LANES = 128
SUBLANES = 8
MXU_COLS = 256
SLABS = D_MODEL // LANES
BLOCKS = D_MODEL // MXU_COLS
SLABS_PER_BLOCK = MXU_COLS // LANES
HALO = 32
SEQ_TILE = 256
CONV_STRIDES = (20, 12)
CONV_GROUP = 4
VMEM_LIMIT_BYTES = 54 * 1024 * 1024

_W_SIZES = (D_MODEL, D_MODEL, D_MODEL, GLA_KEY_DIM, GLA_KEY_DIM, D_MODEL, D_MODEL, D_MODEL, D_MODEL)
_W_BLK = tuple(sum(_W_SIZES[:i]) // MXU_COLS for i in range(len(_W_SIZES)))
(_C_VAL, _C_GATE, _C_Z, _Q, _K, _V, _G_R, _M_CONV, _M_GLA) = range(9)
_RAW = (_C_Z, _G_R, _M_CONV, _M_GLA)
(_R_CZ, _R_GR, _R_MCONV, _R_MGLA) = range(4)

_F32 = jnp.float32
_BF16 = jnp.bfloat16


def _sigmoid(x):
    return 1.0 / (1.0 + jnp.exp(-x))


def _silu(x):
    return x * _sigmoid(x)


_RAW_ACT = (_silu, _silu, _sigmoid, _sigmoid)


def _split3(x):
    p0 = x.astype(_BF16)
    r = x - p0.astype(_F32)
    p1 = r.astype(_BF16)
    p2 = (r - p1.astype(_F32)).astype(_BF16)
    return p0, p1, p2


def _col_blocks(w):
    return w.astype(_BF16).reshape(w.shape[0], -1, MXU_COLS).transpose(1, 0, 2)


def _layer_kernel(x_ref, norm_g_ref, w_main_ref, w_glr_ref, conv_w_ref, conv_b_ref,
                  ln_g_ref, ln_b_ref, w_conv_out_ref, gate_w2_ref, gate_b_ref,
                  gla_g_ref, w_gla_out_ref, w_out_ref, final_g_ref,
                  out_ref,
                  h_s, ubuf, conv_s, raw_s, q_s, k_s, b_s, qe_s, ke_s, kd_s, qb_s, v_s, kv_s, sprev_s,
                  sc_s, o_s, state,
                  *, apply_final_norm):
    ts = x_ref.shape[1]
    n_chunks = ts // CHUNK

    @pl.when(pl.program_id(1) == 0)
    def _():
        ubuf[:, 0:HALO, :] = jnp.zeros((SLABS, HALO, LANES), _F32)
        state[...] = jnp.zeros(state.shape, _F32)

    x = x_ref[0]
    h = x * lax.rsqrt(jnp.mean(x * x, axis=-1, keepdims=True) + EPS) * norm_g_ref[...]
    h_s[...] = h.astype(_BF16)

    def proj_block(blk):
        return jnp.dot(h_s[...], w_main_ref[blk], preferred_element_type=_F32)

    def glu_block(j):
        u = proj_block(_W_BLK[_C_VAL] + j) * _sigmoid(proj_block(_W_BLK[_C_GATE] + j))
        for t in range(SLABS_PER_BLOCK):
            ubuf[j * SLABS_PER_BLOCK + t, HALO:HALO + ts, :] = u[:, t * LANES:(t + 1) * LANES]

    def raw_block(i):
        v_s[i] = proj_block(_W_BLK[_V] + i).astype(_BF16)
        for slot, idx in enumerate(_RAW):
            raw_s[slot, i] = _RAW_ACT[slot](proj_block(_W_BLK[idx] + i))

    def conv_block(i):
        for t in range(SLABS_PER_BLOCK):
            slab = i * SLABS_PER_BLOCK + t
            base = 0
            for stride in CONV_STRIDES:
                for r_lo in range(0, stride, CONV_GROUP):
                    accs = [jnp.broadcast_to(conv_b_ref[slab], (SUBLANES, LANES))] * CONV_GROUP
                    for k in range(CONV_KERNEL):
                        wk = conv_w_ref[slab, k:k + 1, :]
                        for g in range(CONV_GROUP):
                            start = HALO + base + r_lo + g - (CONV_KERNEL - 1) + k
                            win = ubuf[slab, pl.ds(start, SUBLANES, stride=stride), :]
                            accs[g] = accs[g] + wk * win
                    for g in range(CONV_GROUP):
                        conv_s[slab, pl.ds(base + r_lo + g, SUBLANES, stride=stride), :] = accs[g]
                base += SUBLANES * stride

    g_lr = jnp.dot(h_s[...], w_glr_ref[...], preferred_element_type=_F32)
    for j in range(GLA_KEY_DIM // MXU_COLS):
        cols = slice(j * MXU_COLS, (j + 1) * MXU_COLS)
        q_s[:, cols] = proj_block(_W_BLK[_Q] + j)
        k_s[:, cols] = proj_block(_W_BLK[_K] + j)
    z = jnp.dot(g_lr.astype(_BF16), gate_w2_ref[...],
                preferred_element_type=_F32) + gate_b_ref[...]
    log_a = jax.nn.log_sigmoid(z) * (1.0 / GATE_TEMP)
    log_a_pieces = _split3(log_a)

    glu_block(0)
    raw_block(0)

    row = lax.broadcasted_iota(jnp.int32, (ts, ts), 0)
    col = lax.broadcasted_iota(jnp.int32, (ts, ts), 1)
    tril = jnp.where((col <= row) & (col >= (row // CHUNK) * CHUNK), 1.0, 0.0).astype(_BF16)
    b_all = sum(jnp.dot(tril, p, preferred_element_type=_F32) for p in log_a_pieces)
    for hh in range(GLA_HEADS):
        b_s[hh] = b_all[:, hh * GLA_DK:(hh + 1) * GLA_DK]

    for ci in range(n_chunks):
        rows = slice(ci * CHUNK, (ci + 1) * CHUNK)
        b = jnp.concatenate([b_s[hh, rows, :] for hh in range(GLA_HEADS)], axis=1)
        b_mid = b[CHUNK // 2 - 1:CHUNK // 2, :]
        b_last = b[CHUNK - 1:CHUNK, :]
        q = q_s[rows, :] * (GLA_DK ** -0.5)
        k = k_s[rows, :]
        qe = (q * jnp.exp(b - b_mid)).astype(_BF16)
        ke = (k * jnp.exp(b_mid - b)).astype(_BF16)
        kd = (k * jnp.exp(b_last - b)).astype(_BF16)
        qb = (q * jnp.exp(b)).astype(_BF16)
        for hh in range(GLA_HEADS):
            ks = slice(hh * GLA_DK, (hh + 1) * GLA_DK)
            qe_s[hh, rows, :] = qe[:, ks]
            ke_s[hh, rows, :] = ke[:, ks]
            kd_s[hh, rows, :] = kd[:, ks]
            qb_s[hh, rows, :] = qb[:, ks]

    glu_block(1)
    conv_block(0)
    for j in range(1, BLOCKS):
        raw_block(j)
        if j + 1 < BLOCKS:
            glu_block(j + 1)
        conv_block(j)
    ubuf[:, 0:HALO, :] = ubuf[:, ts:ts + HALO, :]

    pairs = [(hh, ci) for hh in range(GLA_HEADS) for ci in range(n_chunks)]
    rows_of = lambda ci: slice(ci * CHUNK, (ci + 1) * CHUNK)
    crow = lax.broadcasted_iota(jnp.int32, (CHUNK, CHUNK), 0)
    ccol = lax.broadcasted_iota(jnp.int32, (CHUNK, CHUNK), 1)
    causal = ccol <= crow
    for hh, ci in pairs:
        rows = rows_of(ci)
        kv_s[hh, ci] = lax.dot_general(kd_s[hh, rows, :], v_s[hh, rows, :],
                                       (((0,), (0,)), ((), ())), preferred_element_type=_F32)
    for hh, ci in pairs:
        rows = rows_of(ci)
        scores = lax.dot_general(qe_s[hh, rows, :], ke_s[hh, rows, :],
                                 (((1,), (1,)), ((), ())), preferred_element_type=_F32)
        sc_s[hh, rows, :] = jnp.where(causal, scores, 0.0).astype(_BF16)
    for hh in range(GLA_HEADS):
        s_prev = state[hh]
        for ci in range(n_chunks):
            sprev_s[hh, ci] = s_prev.astype(_BF16)
            b_last = b_s[hh, ci * CHUNK + CHUNK - 1:(ci + 1) * CHUNK, :]
            decay = jnp.exp(jnp.transpose(jnp.broadcast_to(b_last, (GLA_DK, GLA_DK))))
            s_prev = jnp.concatenate([decay, decay], axis=1) * s_prev + kv_s[hh, ci]
        state[hh] = s_prev
    for hh, ci in pairs:
        rows = rows_of(ci)
        o = jnp.dot(sc_s[hh, rows, :], v_s[hh, rows, :], preferred_element_type=_F32)
        o = o + jnp.dot(qb_s[hh, rows, :], sprev_s[hh, ci], preferred_element_type=_F32)
        o_s[hh, rows, :] = o

    halves = [slice(i * (ts // 2), (i + 1) * (ts // 2)) for i in range(2)]
    y_conv = []
    for rows in halves:
        c = jnp.concatenate([conv_s[s, rows, :] for s in range(SLABS)], axis=1)
        mu = jnp.mean(c, axis=-1, keepdims=True)
        xc = c - mu
        ln = xc * lax.rsqrt(jnp.mean(xc * xc, axis=-1, keepdims=True) + EPS)
        ln = ln * ln_g_ref[...] + ln_b_ref[...]
        cz = jnp.concatenate([raw_s[_R_CZ, j, rows, :] for j in range(BLOCKS)], axis=1)
        a = _silu(ln) * cz
        y_conv.append(jnp.dot(a.astype(_BF16), w_conv_out_ref[...],
                              preferred_element_type=_F32))
    y_gla = []
    for rows in halves:
        gated = []
        for hh in range(GLA_HEADS):
            oh = o_s[hh, rows, :]
            oh = oh * lax.rsqrt(jnp.mean(oh * oh, axis=-1, keepdims=True) + EPS) * gla_g_ref[...]
            gated.append((oh * raw_s[_R_GR, hh, rows, :]).astype(_BF16))
        y_gla.append(jnp.dot(jnp.concatenate(gated, axis=1), w_gla_out_ref[...],
                             preferred_element_type=_F32))
    for i, rows in enumerate(halves):
        m_conv = jnp.concatenate([raw_s[_R_MCONV, j, rows, :] for j in range(BLOCKS)], axis=1)
        m_gla = jnp.concatenate([raw_s[_R_MGLA, j, rows, :] for j in range(BLOCKS)], axis=1)
        y = m_conv * y_conv[i] + m_gla * y_gla[i]
        xn = x_ref[0, rows, :] + jnp.dot(y.astype(_BF16), w_out_ref[...],
                                         preferred_element_type=_F32)
        if apply_final_norm:
            xn = xn * lax.rsqrt(jnp.mean(xn * xn, axis=-1, keepdims=True) + EPS) * final_g_ref[...]
        out_ref[0, rows, :] = xn


def _const_spec(shape):
    zeros = (0,) * len(shape)
    return pl.BlockSpec(shape, lambda b, s: zeros, pipeline_mode=pl.Buffered(1))


def _layer(x, norm_g, w_main, w_glr, conv_w, conv_b, ln_g, ln_b, w_conv_out, gate_w2, gate_b,
           gla_g, w_gla_out, w_out, final_g, *, apply_final_norm):
    batch, seq, d = x.shape
    ts = SEQ_TILE
    assert d == D_MODEL and seq % ts == 0 and ts % CHUNK == 0
    assert HALO >= CONV_KERNEL - 1 and SUBLANES * sum(CONV_STRIDES) == ts
    assert all(s % 8 == 4 and s % CONV_GROUP == 0 for s in CONV_STRIDES)
    assert BLOCKS == GLA_HEADS and GLA_DV == MXU_COLS
    consts = (norm_g, w_main, w_glr, conv_w, conv_b, ln_g, ln_b, w_conv_out, gate_w2, gate_b,
              gla_g, w_gla_out, w_out, final_g)
    x_spec = pl.BlockSpec((1, ts, d), lambda b, s: (b, s, 0))
    head_rows = (GLA_HEADS, ts, GLA_DK)
    return pl.pallas_call(
        functools.partial(_layer_kernel, apply_final_norm=apply_final_norm),
        out_shape=jax.ShapeDtypeStruct(x.shape, x.dtype),
        grid=(batch, seq // ts),
        in_specs=[x_spec] + [_const_spec(c.shape) for c in consts],
        out_specs=x_spec,
        scratch_shapes=[
            pltpu.VMEM((ts, d), _BF16),
            pltpu.VMEM((SLABS, HALO + ts, LANES), _F32),
            pltpu.VMEM((SLABS, ts, LANES), _F32),
            pltpu.VMEM((len(_RAW), BLOCKS, ts, MXU_COLS), _F32),
            pltpu.VMEM((ts, GLA_KEY_DIM), _F32),
            pltpu.VMEM((ts, GLA_KEY_DIM), _F32),
            pltpu.VMEM(head_rows, _F32),
            pltpu.VMEM(head_rows, _BF16),
            pltpu.VMEM(head_rows, _BF16),
            pltpu.VMEM(head_rows, _BF16),
            pltpu.VMEM(head_rows, _BF16),
            pltpu.VMEM((GLA_HEADS, ts, GLA_DV), _BF16),
            pltpu.VMEM((GLA_HEADS, ts // CHUNK, GLA_DK, GLA_DV), _F32),
            pltpu.VMEM((GLA_HEADS, ts // CHUNK, GLA_DK, GLA_DV), _BF16),
            pltpu.VMEM((GLA_HEADS, ts, CHUNK), _BF16),
            pltpu.VMEM((GLA_HEADS, ts, GLA_DV), _F32),
            pltpu.VMEM((GLA_HEADS, GLA_DK, GLA_DV), _F32),
        ],
        compiler_params=pltpu.CompilerParams(
            dimension_semantics=("arbitrary", "arbitrary"),
            vmem_limit_bytes=VMEM_LIMIT_BYTES),
        name="conv_gla_layer",
    )(x, *consts)


def kernel(x, norm_g, w_in, conv_w, conv_b, conv_ln_g, conv_ln_b, w_conv_out, gate_w2, gate_b,
           gla_norm_g, w_gla_out, w_out, final_g):
    depth = w_in.shape[0]
    row = lambda v: v.reshape(1, -1)
    lr0 = 3 * D_MODEL + 2 * GLA_KEY_DIM + D_MODEL
    for l in range(depth):
        w = w_in[l]
        w_main = _col_blocks(jnp.concatenate([w[:, :lr0], w[:, lr0 + GATE_RANK:]], axis=1))
        w_glr = jnp.pad(w[:, lr0:lr0 + GATE_RANK], ((0, 0), (0, LANES - GATE_RANK))).astype(_BF16)
        gate_w2_p = jnp.pad(gate_w2[l], ((0, LANES - GATE_RANK), (0, 0))).astype(_BF16)
        conv_w_slabs = conv_w[l].reshape(CONV_KERNEL, SLABS, LANES).transpose(1, 0, 2)
        conv_b_slabs = conv_b[l].reshape(SLABS, 1, LANES)
        x = _layer(x, row(norm_g[l]), w_main, w_glr, conv_w_slabs, conv_b_slabs,
                   row(conv_ln_g[l]), row(conv_ln_b[l]), w_conv_out[l].astype(_BF16),
                   gate_w2_p, row(gate_b[l]), row(gla_norm_g[l]), w_gla_out[l].astype(_BF16),
                   w_out[l].astype(_BF16), row(final_g), apply_final_norm=(l == depth - 1))
    return x
```

```python
import functools

import jax
import jax.numpy as jnp
from jax import lax
from jax.experimental import pallas as pl
from jax.experimental.pallas import tpu as pltpu

D_MODEL = 1024
CONV_KERNEL = 31
GLA_HEADS = 4
GLA_KEY_DIM = D_MODEL // 2
GLA_DK = GLA_KEY_DIM // GLA_HEADS
GLA_DV = D_MODEL // GLA_HEADS
GATE_RANK = 16
GATE_TEMP = 16.0
CHUNK = 64
EPS = 1e-6

LANES = 128
SUBLANES = 8
MXU_COLS = 256
SLABS = D_MODEL // LANES
BLOCKS = D_MODEL // MXU_COLS
SLABS_PER_BLOCK = MXU_COLS // LANES
HALO = 32
SEQ_TILE = 256
CONV_STRIDES = (20, 12)
CONV_GROUP = 4
VMEM_LIMIT_BYTES = 54 * 1024 * 1024

_W_SIZES = (D_MODEL, D_MODEL, D_MODEL, GLA_KEY_DIM, GLA_KEY_DIM, D_MODEL, D_MODEL, D_MODEL, D_MODEL)
_W_BLK = tuple(sum(_W_SIZES[:i]) // MXU_COLS for i in range(len(_W_SIZES)))
(_C_VAL, _C_GATE, _C_Z, _Q, _K, _V, _G_R, _M_CONV, _M_GLA) = range(9)
_RAW = (_C_Z, _G_R, _M_CONV, _M_GLA)
(_R_CZ, _R_GR, _R_MCONV, _R_MGLA) = range(4)

_F32 = jnp.float32
_BF16 = jnp.bfloat16


LOG2E = 1.4426950408889634


def _sigmoid(x):
    return 1.0 / (1.0 + jnp.exp2(x * (-LOG2E)))


def _silu(x):
    return x * _sigmoid(x)


_RAW_ACT = (_silu, _silu, _sigmoid, _sigmoid)


def _split3(x):
    p0 = x.astype(_BF16)
    r = x - p0.astype(_F32)
    p1 = r.astype(_BF16)
    p2 = (r - p1.astype(_F32)).astype(_BF16)
    return p0, p1, p2


def _col_blocks(w):
    return w.astype(_BF16).reshape(w.shape[0], -1, MXU_COLS).transpose(1, 0, 2)


def _layer_kernel(x_ref, norm_g_ref, w_main_ref, w_glr_ref, conv_w_ref, conv_b_ref,
                  ln_g_ref, ln_b_ref, w_conv_out_ref, gate_w2_ref, gate_b_ref,
                  gla_g_ref, w_gla_out_ref, w_out_ref, final_g_ref,
                  out_ref,
                  h_s, ubuf, conv_s, raw_s, q_s, k_s, b_s, qe_s, ke_s, kd_s, qb_s, v_s, kv_s, sprev_s,
                  sc_s, o_s, state,
                  *, apply_final_norm):
    ts = x_ref.shape[1]
    n_chunks = ts // CHUNK

    @pl.when(pl.program_id(1) == 0)
    def _():
        ubuf[:, 0:HALO, :] = jnp.zeros((SLABS, HALO, LANES), _F32)
        state[...] = jnp.zeros(state.shape, _F32)

    x = x_ref[0]
    h = x * lax.rsqrt(jnp.mean(x * x, axis=-1, keepdims=True) + EPS) * norm_g_ref[...]
    h_s[...] = h.astype(_BF16)

    def proj_block(blk):
        return jnp.dot(h_s[...], w_main_ref[blk], preferred_element_type=_F32)

    def glu_block(j):
        u = proj_block(_W_BLK[_C_VAL] + j) * _sigmoid(proj_block(_W_BLK[_C_GATE] + j))
        for t in range(SLABS_PER_BLOCK):
            ubuf[j * SLABS_PER_BLOCK + t, HALO:HALO + ts, :] = u[:, t * LANES:(t + 1) * LANES]

    def raw_block(i):
        v_s[i] = proj_block(_W_BLK[_V] + i).astype(_BF16)
        for slot, idx in enumerate(_RAW):
            raw_s[slot, i] = _RAW_ACT[slot](proj_block(_W_BLK[idx] + i))

    def conv_block(i):
        for t in range(SLABS_PER_BLOCK):
            slab = i * SLABS_PER_BLOCK + t
            base = 0
            for stride in CONV_STRIDES:
                for r_lo in range(0, stride, CONV_GROUP):
                    accs = [jnp.broadcast_to(conv_b_ref[slab], (SUBLANES, LANES))] * CONV_GROUP
                    for k in range(CONV_KERNEL):
                        wk = conv_w_ref[slab, k:k + 1, :]
                        for g in range(CONV_GROUP):
                            start = HALO + base + r_lo + g - (CONV_KERNEL - 1) + k
                            win = ubuf[slab, pl.ds(start, SUBLANES, stride=stride), :]
                            accs[g] = accs[g] + wk * win
                    for g in range(CONV_GROUP):
                        conv_s[slab, pl.ds(base + r_lo + g, SUBLANES, stride=stride), :] = accs[g]
                base += SUBLANES * stride

    for j in range(BLOCKS):
        glu_block(j)
        conv_block(j)

    g_lr = jnp.dot(h_s[...], w_glr_ref[...], preferred_element_type=_F32)
    for j in range(GLA_KEY_DIM // MXU_COLS):
        cols = slice(j * MXU_COLS, (j + 1) * MXU_COLS)
        q_s[:, cols] = proj_block(_W_BLK[_Q] + j)
        k_s[:, cols] = proj_block(_W_BLK[_K] + j)
    z = jnp.dot(g_lr.astype(_BF16), gate_w2_ref[...],
                preferred_element_type=_F32) + gate_b_ref[...]
    log_a = jax.nn.log_sigmoid(z) * (1.0 / GATE_TEMP)
    log_a_pieces = _split3(log_a)

    for j in range(BLOCKS - 1):
        raw_block(j)

    row = lax.broadcasted_iota(jnp.int32, (ts, ts), 0)
    col = lax.broadcasted_iota(jnp.int32, (ts, ts), 1)
    tril = jnp.where((col <= row) & (col >= (row // CHUNK) * CHUNK), 1.0, 0.0).astype(_BF16)
    b_all = sum(jnp.dot(tril, p, preferred_element_type=_F32) for p in log_a_pieces)
    b_all = b_all * LOG2E
    for hh in range(GLA_HEADS):
        b_s[hh] = b_all[:, hh * GLA_DK:(hh + 1) * GLA_DK]

    for ci in range(n_chunks):
        rows = slice(ci * CHUNK, (ci + 1) * CHUNK)
        b = jnp.concatenate([b_s[hh, rows, :] for hh in range(GLA_HEADS)], axis=1)
        b_mid = b[CHUNK // 2 - 1:CHUNK // 2, :]
        b_last = b[CHUNK - 1:CHUNK, :]
        q = q_s[rows, :] * (GLA_DK ** -0.5)
        k = k_s[rows, :]
        qe = (q * jnp.exp2(b - b_mid)).astype(_BF16)
        ke = (k * jnp.exp2(b_mid - b)).astype(_BF16)
        kd = (k * jnp.exp2(b_last - b)).astype(_BF16)
        qb = (q * jnp.exp2(b)).astype(_BF16)
        for hh in range(GLA_HEADS):
            ks = slice(hh * GLA_DK, (hh + 1) * GLA_DK)
            qe_s[hh, rows, :] = qe[:, ks]
            ke_s[hh, rows, :] = ke[:, ks]
            kd_s[hh, rows, :] = kd[:, ks]
            qb_s[hh, rows, :] = qb[:, ks]

    raw_block(BLOCKS - 1)
    ubuf[:, 0:HALO, :] = ubuf[:, ts:ts + HALO, :]

    pairs = [(hh, ci) for hh in range(GLA_HEADS) for ci in range(n_chunks)]
    rows_of = lambda ci: slice(ci * CHUNK, (ci + 1) * CHUNK)
    crow = lax.broadcasted_iota(jnp.int32, (CHUNK, CHUNK), 0)
    ccol = lax.broadcasted_iota(jnp.int32, (CHUNK, CHUNK), 1)
    causal = ccol <= crow
    for hh, ci in pairs:
        rows = rows_of(ci)
        kv_s[hh, ci] = lax.dot_general(kd_s[hh, rows, :], v_s[hh, rows, :],
                                       (((0,), (0,)), ((), ())), preferred_element_type=_F32)
    for hh, ci in pairs:
        rows = rows_of(ci)
        scores = lax.dot_general(qe_s[hh, rows, :], ke_s[hh, rows, :],
                                 (((1,), (1,)), ((), ())), preferred_element_type=_F32)
        sc_s[hh, rows, :] = jnp.where(causal, scores, 0.0).astype(_BF16)
    for hh in range(GLA_HEADS):
        s_prev = state[hh]
        for ci in range(n_chunks):
            sprev_s[hh, ci] = s_prev.astype(_BF16)
            b_last = b_s[hh, ci * CHUNK + CHUNK - 1:(ci + 1) * CHUNK, :]
            decay = jnp.exp2(jnp.transpose(jnp.broadcast_to(b_last, (GLA_DK, GLA_DK))))
            s_prev = jnp.concatenate([decay, decay], axis=1) * s_prev + kv_s[hh, ci]
        state[hh] = s_prev
    for hh, ci in pairs:
        rows = rows_of(ci)
        o = jnp.dot(sc_s[hh, rows, :], v_s[hh, rows, :], preferred_element_type=_F32)
        o = o + jnp.dot(qb_s[hh, rows, :], sprev_s[hh, ci], preferred_element_type=_F32)
        o_s[hh, rows, :] = o

    halves = [slice(i * (ts // 2), (i + 1) * (ts // 2)) for i in range(2)]
    y_conv = []
    for rows in halves:
        c = jnp.concatenate([conv_s[s, rows, :] for s in range(SLABS)], axis=1)
        mu = jnp.mean(c, axis=-1, keepdims=True)
        xc = c - mu
        ln = xc * lax.rsqrt(jnp.mean(xc * xc, axis=-1, keepdims=True) + EPS)
        ln = ln * ln_g_ref[...] + ln_b_ref[...]
        cz = jnp.concatenate([raw_s[_R_CZ, j, rows, :] for j in range(BLOCKS)], axis=1)
        a = _silu(ln) * cz
        y_conv.append(jnp.dot(a.astype(_BF16), w_conv_out_ref[...],
                              preferred_element_type=_F32))
    y_gla = []
    for rows in halves:
        gated = []
        for hh in range(GLA_HEADS):
            oh = o_s[hh, rows, :]
            oh = oh * lax.rsqrt(jnp.mean(oh * oh, axis=-1, keepdims=True) + EPS) * gla_g_ref[...]
            gated.append((oh * raw_s[_R_GR, hh, rows, :]).astype(_BF16))
        y_gla.append(jnp.dot(jnp.concatenate(gated, axis=1), w_gla_out_ref[...],
                             preferred_element_type=_F32))
    for i, rows in enumerate(halves):
        m_conv = jnp.concatenate([raw_s[_R_MCONV, j, rows, :] for j in range(BLOCKS)], axis=1)
        m_gla = jnp.concatenate([raw_s[_R_MGLA, j, rows, :] for j in range(BLOCKS)], axis=1)
        y = m_conv * y_conv[i] + m_gla * y_gla[i]
        xn = x_ref[0, rows, :] + jnp.dot(y.astype(_BF16), w_out_ref[...],
                                         preferred_element_type=_F32)
        if apply_final_norm:
            xn = xn * lax.rsqrt(jnp.mean(xn * xn, axis=-1, keepdims=True) + EPS) * final_g_ref[...]
        out_ref[0, rows, :] = xn


def _const_spec(shape):
    zeros = (0,) * len(shape)
    return pl.BlockSpec(shape, lambda b, s: zeros, pipeline_mode=pl.Buffered(1))


def _layer(x, norm_g, w_main, w_glr, conv_w, conv_b, ln_g, ln_b, w_conv_out, gate_w2, gate_b,
           gla_g, w_gla_out, w_out, final_g, *, apply_final_norm):
    batch, seq, d = x.shape
    ts = SEQ_TILE
    assert d == D_MODEL and seq % ts == 0 and ts % CHUNK == 0
    assert HALO >= CONV_KERNEL - 1 and SUBLANES * sum(CONV_STRIDES) == ts
    assert all(s % 8 == 4 and s % CONV_GROUP == 0 for s in CONV_STRIDES)
    assert BLOCKS == GLA_HEADS and GLA_DV == MXU_COLS
    consts = (norm_g, w_main, w_glr, conv_w, conv_b, ln_g, ln_b, w_conv_out, gate_w2, gate_b,
              gla_g, w_gla_out, w_out, final_g)
    x_spec = pl.BlockSpec((1, ts, d), lambda b, s: (b, s, 0))
    head_rows = (GLA_HEADS, ts, GLA_DK)
    return pl.pallas_call(
        functools.partial(_layer_kernel, apply_final_norm=apply_final_norm),
        out_shape=jax.ShapeDtypeStruct(x.shape, x.dtype),
        grid=(batch, seq // ts),
        in_specs=[x_spec] + [_const_spec(c.shape) for c in consts],
        out_specs=x_spec,
        scratch_shapes=[
            pltpu.VMEM((ts, d), _BF16),
            pltpu.VMEM((SLABS, HALO + ts, LANES), _F32),
            pltpu.VMEM((SLABS, ts, LANES), _F32),
            pltpu.VMEM((len(_RAW), BLOCKS, ts, MXU_COLS), _F32),
            pltpu.VMEM((ts, GLA_KEY_DIM), _F32),
            pltpu.VMEM((ts, GLA_KEY_DIM), _F32),
            pltpu.VMEM(head_rows, _F32),
            pltpu.VMEM(head_rows, _BF16),
            pltpu.VMEM(head_rows, _BF16),
            pltpu.VMEM(head_rows, _BF16),
            pltpu.VMEM(head_rows, _BF16),
            pltpu.VMEM((GLA_HEADS, ts, GLA_DV), _BF16),
            pltpu.VMEM((GLA_HEADS, ts // CHUNK, GLA_DK, GLA_DV), _F32),
            pltpu.VMEM((GLA_HEADS, ts // CHUNK, GLA_DK, GLA_DV), _BF16),
            pltpu.VMEM((GLA_HEADS, ts, CHUNK), _BF16),
            pltpu.VMEM((GLA_HEADS, ts, GLA_DV), _F32),
            pltpu.VMEM((GLA_HEADS, GLA_DK, GLA_DV), _F32),
        ],
        compiler_params=pltpu.CompilerParams(
            dimension_semantics=("arbitrary", "arbitrary"),
            vmem_limit_bytes=VMEM_LIMIT_BYTES),
        name="conv_gla_layer",
    )(x, *consts)


def kernel(x, norm_g, w_in, conv_w, conv_b, conv_ln_g, conv_ln_b, w_conv_out, gate_w2, gate_b,
           gla_norm_g, w_gla_out, w_out, final_g):
    depth = w_in.shape[0]
    row = lambda v: v.reshape(1, -1)
    lr0 = 3 * D_MODEL + 2 * GLA_KEY_DIM + D_MODEL
    for l in range(depth):
        w = w_in[l]
        w_main = _col_blocks(jnp.concatenate([w[:, :lr0], w[:, lr0 + GATE_RANK:]], axis=1))
        w_glr = jnp.pad(w[:, lr0:lr0 + GATE_RANK], ((0, 0), (0, LANES - GATE_RANK))).astype(_BF16)
        gate_w2_p = jnp.pad(gate_w2[l], ((0, LANES - GATE_RANK), (0, 0))).astype(_BF16)
        conv_w_slabs = conv_w[l].reshape(CONV_KERNEL, SLABS, LANES).transpose(1, 0, 2)
        conv_b_slabs = conv_b[l].reshape(SLABS, 1, LANES)
        x = _layer(x, row(norm_g[l]), w_main, w_glr, conv_w_slabs, conv_b_slabs,
                   row(conv_ln_g[l]), row(conv_ln_b[l]), w_conv_out[l].astype(_BF16),
                   gate_w2_p, row(gate_b[l]), row(gla_norm_g[l]), w_gla_out[l].astype(_BF16),
                   w_out[l].astype(_BF16), row(final_g), apply_final_norm=(l == depth - 1))
    return x
```

```python
import functools

import jax
import jax.numpy as jnp
from jax import lax
from jax.experimental import pallas as pl
from jax.experimental.pallas import tpu as pltpu

D_MODEL = 1024
CONV_KERNEL = 31
GLA_HEADS = 4
GLA_KEY_DIM = D_MODEL // 2
GLA_DK = GLA_KEY_DIM // GLA_HEADS
GLA_DV = D_MODEL // GLA_HEADS
GATE_RANK = 16
GATE_TEMP = 16.0
CHUNK = 64
EPS = 1e-6

LANES = 128
SUBLANES = 8
MXU_COLS = 256
SLABS = D_MODEL // LANES
BLOCKS = D_MODEL // MXU_COLS
SLABS_PER_BLOCK = MXU_COLS // LANES
HALO = 32
SEQ_TILE = 256
CONV_STRIDES = (20, 12)
CONV_GROUP = 4
VMEM_LIMIT_BYTES = 54 * 1024 * 1024

_W_SIZES = (D_MODEL, D_MODEL, D_MODEL, GLA_KEY_DIM, GLA_KEY_DIM, D_MODEL, D_MODEL, D_MODEL, D_MODEL)
_W_BLK = tuple(sum(_W_SIZES[:i]) // MXU_COLS for i in range(len(_W_SIZES)))
(_C_VAL, _C_GATE, _C_Z, _Q, _K, _V, _G_R, _M_CONV, _M_GLA) = range(9)
_RAW = (_C_Z, _G_R, _M_CONV, _M_GLA)
(_R_CZ, _R_GR, _R_MCONV, _R_MGLA) = range(4)

_F32 = jnp.float32
_BF16 = jnp.bfloat16


LOG2E = 1.4426950408889634


def _sigmoid(x):
    return 1.0 / (1.0 + jnp.exp2(x * (-LOG2E)))


def _silu(x):
    return x * _sigmoid(x)


_RAW_ACT = (_silu, _silu, _sigmoid, _sigmoid)


def _split3(x):
    p0 = x.astype(_BF16)
    r = x - p0.astype(_F32)
    p1 = r.astype(_BF16)
    p2 = (r - p1.astype(_F32)).astype(_BF16)
    return p0, p1, p2


def _col_blocks(w):
    return w.astype(_BF16).reshape(w.shape[0], -1, MXU_COLS).transpose(1, 0, 2)


def _layer_kernel(x_ref, norm_g_ref, w_lo_ref, w_hi_ref, w_glr_ref, tril_ref, conv_w_ref, conv_b_ref,
                  ln_g_ref, ln_b_ref, w_conv_out_ref, gate_w2_ref, gate_b_ref,
                  gla_g_ref, w_gla_out_ref, w_out_ref, final_g_ref,
                  out_ref,
                  h_s, ubuf, conv_s, raw_s, q_s, k_s, b_s, qe_s, ke_s, kd_s, qb_s, v_s, kv_s, sprev_s,
                  sc_s, o_s, state,
                  *, apply_final_norm):
    ts = x_ref.shape[1]
    n_chunks = ts // CHUNK

    @pl.when(pl.program_id(1) == 0)
    def _():
        ubuf[:, 0:HALO, :] = jnp.zeros((SLABS, HALO, LANES), _F32)
        state[...] = jnp.zeros(state.shape, _F32)

    x = x_ref[0]
    h = x * lax.rsqrt(jnp.mean(x * x, axis=-1, keepdims=True) + EPS) * norm_g_ref[...]
    h_s[...] = h.astype(_BF16)

    def proj_block(blk):
        w_blk = w_lo_ref[blk] if blk < _W_BLK[_G_R] else w_hi_ref[blk - _W_BLK[_G_R]]
        return jnp.dot(h_s[...], w_blk, preferred_element_type=_F32)

    def glu_block(j):
        u = proj_block(_W_BLK[_C_VAL] + j) * _sigmoid(proj_block(_W_BLK[_C_GATE] + j))
        for t in range(SLABS_PER_BLOCK):
            ubuf[j * SLABS_PER_BLOCK + t, HALO:HALO + ts, :] = u[:, t * LANES:(t + 1) * LANES]

    def raw_block(i):
        v_s[i] = proj_block(_W_BLK[_V] + i).astype(_BF16)
        for slot, idx in enumerate(_RAW):
            raw_s[slot, i] = _RAW_ACT[slot](proj_block(_W_BLK[idx] + i))

    def conv_block(i):
        for t in range(SLABS_PER_BLOCK):
            slab = i * SLABS_PER_BLOCK + t
            base = 0
            for stride in CONV_STRIDES:
                for r_lo in range(0, stride, CONV_GROUP):
                    accs = [jnp.broadcast_to(conv_b_ref[slab], (SUBLANES, LANES))] * CONV_GROUP
                    for k in range(CONV_KERNEL):
                        wk = conv_w_ref[slab, k:k + 1, :]
                        for g in range(CONV_GROUP):
                            start = HALO + base + r_lo + g - (CONV_KERNEL - 1) + k
                            win = ubuf[slab, pl.ds(start, SUBLANES, stride=stride), :]
                            accs[g] = accs[g] + wk * win
                    for g in range(CONV_GROUP):
                        conv_s[slab, pl.ds(base + r_lo + g, SUBLANES, stride=stride), :] = accs[g]
                base += SUBLANES * stride

    for j in range(BLOCKS):
        glu_block(j)
        conv_block(j)

    g_lr = jnp.dot(h_s[...], w_glr_ref[...], preferred_element_type=_F32)
    for j in range(GLA_KEY_DIM // MXU_COLS):
        cols = slice(j * MXU_COLS, (j + 1) * MXU_COLS)
        q_s[:, cols] = proj_block(_W_BLK[_Q] + j)
        k_s[:, cols] = proj_block(_W_BLK[_K] + j)
    z = jnp.dot(g_lr.astype(_BF16), gate_w2_ref[...],
                preferred_element_type=_F32) + gate_b_ref[...]
    log_a = jax.nn.log_sigmoid(z) * (1.0 / GATE_TEMP)
    log_a_pieces = _split3(log_a)

    for j in range(BLOCKS - 1):
        raw_block(j)

    b_all = sum(jnp.dot(tril_ref[...], p, preferred_element_type=_F32) for p in log_a_pieces)
    b_all = b_all * LOG2E
    for hh in range(GLA_HEADS):
        b_s[hh] = b_all[:, hh * GLA_DK:(hh + 1) * GLA_DK]

    for ci in range(n_chunks):
        rows = slice(ci * CHUNK, (ci + 1) * CHUNK)
        b = jnp.concatenate([b_s[hh, rows, :] for hh in range(GLA_HEADS)], axis=1)
        b_mid = b[CHUNK // 2 - 1:CHUNK // 2, :]
        b_last = b[CHUNK - 1:CHUNK, :]
        q = q_s[rows, :] * (GLA_DK ** -0.5)
        k = k_s[rows, :]
        qe = (q * jnp.exp2(b - b_mid)).astype(_BF16)
        ke = (k * jnp.exp2(b_mid - b)).astype(_BF16)
        kd = (k * jnp.exp2(b_last - b)).astype(_BF16)
        qb = (q * jnp.exp2(b)).astype(_BF16)
        for hh in range(GLA_HEADS):
            ks = slice(hh * GLA_DK, (hh + 1) * GLA_DK)
            qe_s[hh, rows, :] = qe[:, ks]
            ke_s[hh, rows, :] = ke[:, ks]
            kd_s[hh, rows, :] = kd[:, ks]
            qb_s[hh, rows, :] = qb[:, ks]

    raw_block(BLOCKS - 1)
    ubuf[:, 0:HALO, :] = ubuf[:, ts:ts + HALO, :]

    pairs = [(hh, ci) for hh in range(GLA_HEADS) for ci in range(n_chunks)]
    rows_of = lambda ci: slice(ci * CHUNK, (ci + 1) * CHUNK)
    crow = lax.broadcasted_iota(jnp.int32, (CHUNK, CHUNK), 0)
    ccol = lax.broadcasted_iota(jnp.int32, (CHUNK, CHUNK), 1)
    causal = ccol <= crow
    for hh, ci in pairs:
        rows = rows_of(ci)
        kv_s[hh, ci] = lax.dot_general(kd_s[hh, rows, :], v_s[hh, rows, :],
                                       (((0,), (0,)), ((), ())), preferred_element_type=_F32)
    for hh, ci in pairs:
        rows = rows_of(ci)
        scores = lax.dot_general(qe_s[hh, rows, :], ke_s[hh, rows, :],
                                 (((1,), (1,)), ((), ())), preferred_element_type=_F32)
        sc_s[hh, rows, :] = jnp.where(causal, scores, 0.0).astype(_BF16)
    for hh in range(GLA_HEADS):
        s_prev = state[hh]
        for ci in range(n_chunks):
            sprev_s[hh, ci] = s_prev.astype(_BF16)
            b_last = b_s[hh, ci * CHUNK + CHUNK - 1:(ci + 1) * CHUNK, :]
            decay = jnp.exp2(jnp.transpose(jnp.broadcast_to(b_last, (GLA_DK, GLA_DK))))
            s_prev = jnp.concatenate([decay, decay], axis=1) * s_prev + kv_s[hh, ci]
        state[hh] = s_prev
    for hh, ci in pairs:
        rows = rows_of(ci)
        o = jnp.dot(sc_s[hh, rows, :], v_s[hh, rows, :], preferred_element_type=_F32)
        o = o + jnp.dot(qb_s[hh, rows, :], sprev_s[hh, ci], preferred_element_type=_F32)
        o_s[hh, rows, :] = o

    halves = [slice(i * (ts // 2), (i + 1) * (ts // 2)) for i in range(2)]

    def block_dots(lhs, w_ref):
        return [jnp.dot(lhs, w_ref[j], preferred_element_type=_F32) for j in range(BLOCKS)]

    y_conv = []
    for rows in halves:
        c = jnp.concatenate([conv_s[s, rows, :] for s in range(SLABS)], axis=1)
        mu = jnp.mean(c, axis=-1, keepdims=True)
        xc = c - mu
        ln = xc * lax.rsqrt(jnp.mean(xc * xc, axis=-1, keepdims=True) + EPS)
        ln = ln * ln_g_ref[...] + ln_b_ref[...]
        cz = jnp.concatenate([raw_s[_R_CZ, j, rows, :] for j in range(BLOCKS)], axis=1)
        a = _silu(ln) * cz
        y_conv.append(block_dots(a.astype(_BF16), w_conv_out_ref))
    y_gla = []
    for rows in halves:
        gated = []
        for hh in range(GLA_HEADS):
            oh = o_s[hh, rows, :]
            oh = oh * lax.rsqrt(jnp.mean(oh * oh, axis=-1, keepdims=True) + EPS) * gla_g_ref[...]
            gated.append((oh * raw_s[_R_GR, hh, rows, :]).astype(_BF16))
        y_gla.append(block_dots(jnp.concatenate(gated, axis=1), w_gla_out_ref))
    for i, rows in enumerate(halves):
        y = jnp.concatenate([raw_s[_R_MCONV, j, rows, :] * y_conv[i][j]
                             + raw_s[_R_MGLA, j, rows, :] * y_gla[i][j]
                             for j in range(BLOCKS)], axis=1)
        xn = x_ref[0, rows, :] + jnp.concatenate(block_dots(y.astype(_BF16), w_out_ref), axis=1)
        if apply_final_norm:
            xn = xn * lax.rsqrt(jnp.mean(xn * xn, axis=-1, keepdims=True) + EPS) * final_g_ref[...]
        out_ref[0, rows, :] = xn


def _const_spec(shape):
    zeros = (0,) * len(shape)
    return pl.BlockSpec(shape, lambda b, s: zeros, pipeline_mode=pl.Buffered(1))


def _layer(x, norm_g, w_lo, w_hi, w_glr, tril, conv_w, conv_b, ln_g, ln_b, w_conv_out, gate_w2, gate_b,
           gla_g, w_gla_out, w_out, final_g, *, apply_final_norm):
    batch, seq, d = x.shape
    ts = SEQ_TILE
    assert d == D_MODEL and seq % ts == 0 and ts % CHUNK == 0
    assert HALO >= CONV_KERNEL - 1 and SUBLANES * sum(CONV_STRIDES) == ts
    assert all(s % 8 == 4 and s % CONV_GROUP == 0 for s in CONV_STRIDES)
    assert BLOCKS == GLA_HEADS and GLA_DV == MXU_COLS
    consts = (norm_g, w_lo, w_hi, w_glr, tril, conv_w, conv_b, ln_g, ln_b, w_conv_out, gate_w2, gate_b,
              gla_g, w_gla_out, w_out, final_g)
    x_spec = pl.BlockSpec((1, ts, d), lambda b, s: (b, s, 0))
    head_rows = (GLA_HEADS, ts, GLA_DK)
    return pl.pallas_call(
        functools.partial(_layer_kernel, apply_final_norm=apply_final_norm),
        out_shape=jax.ShapeDtypeStruct(x.shape, x.dtype),
        grid=(batch, seq // ts),
        in_specs=[x_spec] + [_const_spec(c.shape) for c in consts],
        out_specs=x_spec,
        scratch_shapes=[
            pltpu.VMEM((ts, d), _BF16),
            pltpu.VMEM((SLABS, HALO + ts, LANES), _F32),
            pltpu.VMEM((SLABS, ts, LANES), _F32),
            pltpu.VMEM((len(_RAW), BLOCKS, ts, MXU_COLS), _F32),
            pltpu.VMEM((ts, GLA_KEY_DIM), _F32),
            pltpu.VMEM((ts, GLA_KEY_DIM), _F32),
            pltpu.VMEM(head_rows, _F32),
            pltpu.VMEM(head_rows, _BF16),
            pltpu.VMEM(head_rows, _BF16),
            pltpu.VMEM(head_rows, _BF16),
            pltpu.VMEM(head_rows, _BF16),
            pltpu.VMEM((GLA_HEADS, ts, GLA_DV), _BF16),
            pltpu.VMEM((GLA_HEADS, ts // CHUNK, GLA_DK, GLA_DV), _F32),
            pltpu.VMEM((GLA_HEADS, ts // CHUNK, GLA_DK, GLA_DV), _BF16),
            pltpu.VMEM((GLA_HEADS, ts, CHUNK), _BF16),
            pltpu.VMEM((GLA_HEADS, ts, GLA_DV), _F32),
            pltpu.VMEM((GLA_HEADS, GLA_DK, GLA_DV), _F32),
        ],
        compiler_params=pltpu.CompilerParams(
            dimension_semantics=("arbitrary", "arbitrary"),
            vmem_limit_bytes=VMEM_LIMIT_BYTES),
        name="conv_gla_layer",
    )(x, *consts)


def kernel(x, norm_g, w_in, conv_w, conv_b, conv_ln_g, conv_ln_b, w_conv_out, gate_w2, gate_b,
           gla_norm_g, w_gla_out, w_out, final_g):
    depth = w_in.shape[0]
    t = jnp.arange(SEQ_TILE)
    tril = ((t[None, :] <= t[:, None]) & (t[None, :] // CHUNK == t[:, None] // CHUNK)).astype(_BF16)
    row = lambda v: v.reshape(1, -1)
    lr0 = 3 * D_MODEL + 2 * GLA_KEY_DIM + D_MODEL
    for l in range(depth):
        w = w_in[l]
        w_lo = _col_blocks(w[:, :lr0])
        w_hi = _col_blocks(w[:, lr0 + GATE_RANK:])
        w_glr = jnp.pad(w[:, lr0:lr0 + GATE_RANK], ((0, 0), (0, LANES - GATE_RANK))).astype(_BF16)
        gate_w2_p = jnp.pad(gate_w2[l], ((0, LANES - GATE_RANK), (0, 0))).astype(_BF16)
        conv_w_slabs = conv_w[l].reshape(CONV_KERNEL, SLABS, LANES).transpose(1, 0, 2)
        conv_b_slabs = conv_b[l].reshape(SLABS, 1, LANES)
        x = _layer(x, row(norm_g[l]), w_lo, w_hi, w_glr, tril, conv_w_slabs, conv_b_slabs,
                   row(conv_ln_g[l]), row(conv_ln_b[l]), _col_blocks(w_conv_out[l]),
                   gate_w2_p, row(gate_b[l]), row(gla_norm_g[l]), _col_blocks(w_gla_out[l]),
                   _col_blocks(w_out[l]), row(final_g), apply_final_norm=(l == depth - 1))
    return x
```

```python
import functools

import jax
import jax.numpy as jnp
from jax import lax
from jax.experimental import pallas as pl
from jax.experimental.pallas import tpu as pltpu

D_MODEL = 1024
CONV_KERNEL = 31
GLA_HEADS = 4
GLA_KEY_DIM = D_MODEL // 2
GLA_DK = GLA_KEY_DIM // GLA_HEADS
GLA_DV = D_MODEL // GLA_HEADS
GATE_RANK = 16
GATE_TEMP = 16.0
CHUNK = 64
EPS = 1e-6

LANES = 128
SUBLANES = 8
MXU_COLS = 256
SLABS = D_MODEL // LANES
BLOCKS = D_MODEL // MXU_COLS
SLABS_PER_BLOCK = MXU_COLS // LANES
HALO = 32
SEQ_TILE = 256
CONV_STRIDES = (20, 12)
CONV_GROUP = 4
VMEM_LIMIT_BYTES = 54 * 1024 * 1024

_W_SIZES = (D_MODEL, D_MODEL, D_MODEL, GLA_KEY_DIM, GLA_KEY_DIM, D_MODEL, D_MODEL, D_MODEL, D_MODEL)
_W_BLK = tuple(sum(_W_SIZES[:i]) // MXU_COLS for i in range(len(_W_SIZES)))
(_C_VAL, _C_GATE, _C_Z, _Q, _K, _V, _G_R, _M_CONV, _M_GLA) = range(9)
_RAW = (_C_Z, _G_R, _M_CONV, _M_GLA)
(_R_CZ, _R_GR, _R_MCONV, _R_MGLA) = range(4)

_F32 = jnp.float32
_BF16 = jnp.bfloat16

LOG2E = 1.4426950408889634


def _sigmoid(x):
    return 1.0 / (1.0 + jnp.exp2(x * (-LOG2E)))


def _silu(x):
    return x * _sigmoid(x)


_RAW_ACT = (_silu, _silu, _sigmoid, _sigmoid)


def _split3(x):
    p0 = x.astype(_BF16)
    r = x - p0.astype(_F32)
    p1 = r.astype(_BF16)
    p2 = (r - p1.astype(_F32)).astype(_BF16)
    return p0, p1, p2


def _layer_kernel(x_ref, norm_g_ref, w_glr_ref, tril_ref, conv_w_ref, conv_b_ref,
                  ln_g_ref, ln_b_ref, gate_w2_ref, gate_b_ref, gla_g_ref, final_g_ref,
                  *refs, apply_final_norm):
    n_proj = sum(_W_SIZES) // MXU_COLS
    w_proj = refs[:n_proj]
    w_conv_out, w_gla_out, w_out = (refs[n_proj + i * BLOCKS:n_proj + (i + 1) * BLOCKS]
                                    for i in range(3))
    (out_ref, h_s, ubuf, conv_s, raw_s, q_s, k_s, b_s, qe_s, ke_s, kd_s, qb_s, v_s, kv_s, sprev_s,
     sc_s, o_s, state) = refs[n_proj + 3 * BLOCKS:]
    ts = x_ref.shape[1]
    n_chunks = ts // CHUNK

    @pl.when(pl.program_id(1) == 0)
    def _():
        ubuf[:, 0:HALO, :] = jnp.zeros((SLABS, HALO, LANES), _F32)
        state[...] = jnp.zeros(state.shape, _F32)

    x = x_ref[0]
    h = x * lax.rsqrt(jnp.mean(x * x, axis=-1, keepdims=True) + EPS) * norm_g_ref[...]
    h_s[...] = h.astype(_BF16)

    def proj_block(blk):
        return jnp.dot(h_s[...], w_proj[blk][...], preferred_element_type=_F32)

    def glu_block(j):
        u = proj_block(_W_BLK[_C_VAL] + j) * _sigmoid(proj_block(_W_BLK[_C_GATE] + j))
        for t in range(SLABS_PER_BLOCK):
            ubuf[j * SLABS_PER_BLOCK + t, HALO:HALO + ts, :] = u[:, t * LANES:(t + 1) * LANES]

    def raw_block(i):
        v_s[i] = proj_block(_W_BLK[_V] + i).astype(_BF16)
        for slot, idx in enumerate(_RAW):
            raw_s[slot, i] = _RAW_ACT[slot](proj_block(_W_BLK[idx] + i))

    def conv_block(i):
        for t in range(SLABS_PER_BLOCK):
            slab = i * SLABS_PER_BLOCK + t
            base = 0
            for stride in CONV_STRIDES:
                for r_lo in range(0, stride, CONV_GROUP):
                    accs = [jnp.broadcast_to(conv_b_ref[slab], (SUBLANES, LANES))] * CONV_GROUP
                    for k in range(CONV_KERNEL):
                        wk = conv_w_ref[slab, k:k + 1, :]
                        for g in range(CONV_GROUP):
                            start = HALO + base + r_lo + g - (CONV_KERNEL - 1) + k
                            win = ubuf[slab, pl.ds(start, SUBLANES, stride=stride), :]
                            accs[g] = accs[g] + wk * win
                    for g in range(CONV_GROUP):
                        conv_s[slab, pl.ds(base + r_lo + g, SUBLANES, stride=stride), :] = accs[g]
                base += SUBLANES * stride

    for j in range(BLOCKS):
        glu_block(j)
        conv_block(j)

    g_lr = jnp.dot(h_s[...], w_glr_ref[...], preferred_element_type=_F32)
    for j in range(GLA_KEY_DIM // MXU_COLS):
        cols = slice(j * MXU_COLS, (j + 1) * MXU_COLS)
        q_s[:, cols] = proj_block(_W_BLK[_Q] + j)
        k_s[:, cols] = proj_block(_W_BLK[_K] + j)
    z = jnp.dot(g_lr.astype(_BF16), gate_w2_ref[...],
                preferred_element_type=_F32) + gate_b_ref[...]
    log_a = jax.nn.log_sigmoid(z) * (1.0 / GATE_TEMP)
    log_a_pieces = _split3(log_a)

    for j in range(BLOCKS - 1):
        raw_block(j)

    b_all = sum(jnp.dot(tril_ref[...], p, preferred_element_type=_F32) for p in log_a_pieces)
    b_all = b_all * LOG2E
    for hh in range(GLA_HEADS):
        b_s[hh] = b_all[:, hh * GLA_DK:(hh + 1) * GLA_DK]

    for ci in range(n_chunks):
        rows = slice(ci * CHUNK, (ci + 1) * CHUNK)
        b = jnp.concatenate([b_s[hh, rows, :] for hh in range(GLA_HEADS)], axis=1)
        b_mid = b[CHUNK // 2 - 1:CHUNK // 2, :]
        b_last = b[CHUNK - 1:CHUNK, :]
        q = q_s[rows, :] * (GLA_DK ** -0.5)
        k = k_s[rows, :]
        qe = (q * jnp.exp2(b - b_mid)).astype(_BF16)
        ke = (k * jnp.exp2(b_mid - b)).astype(_BF16)
        kd = (k * jnp.exp2(b_last - b)).astype(_BF16)
        qb = (q * jnp.exp2(b)).astype(_BF16)
        for hh in range(GLA_HEADS):
            ks = slice(hh * GLA_DK, (hh + 1) * GLA_DK)
            qe_s[hh, rows, :] = qe[:, ks]
            ke_s[hh, rows, :] = ke[:, ks]
            kd_s[hh, rows, :] = kd[:, ks]
            qb_s[hh, rows, :] = qb[:, ks]

    raw_block(BLOCKS - 1)
    ubuf[:, 0:HALO, :] = ubuf[:, ts:ts + HALO, :]

    pairs = [(hh, ci) for hh in range(GLA_HEADS) for ci in range(n_chunks)]
    rows_of = lambda ci: slice(ci * CHUNK, (ci + 1) * CHUNK)
    crow = lax.broadcasted_iota(jnp.int32, (CHUNK, CHUNK), 0)
    ccol = lax.broadcasted_iota(jnp.int32, (CHUNK, CHUNK), 1)
    causal = ccol <= crow
    for hh, ci in pairs:
        rows = rows_of(ci)
        kv_s[hh, ci] = lax.dot_general(kd_s[hh, rows, :], v_s[hh, rows, :],
                                       (((0,), (0,)), ((), ())), preferred_element_type=_F32)
    for hh, ci in pairs:
        rows = rows_of(ci)
        scores = lax.dot_general(qe_s[hh, rows, :], ke_s[hh, rows, :],
                                 (((1,), (1,)), ((), ())), preferred_element_type=_F32)
        sc_s[hh, rows, :] = jnp.where(causal, scores, 0.0).astype(_BF16)
    for hh in range(GLA_HEADS):
        s_prev = state[hh]
        for ci in range(n_chunks):
            sprev_s[hh, ci] = s_prev.astype(_BF16)
            b_last = b_s[hh, ci * CHUNK + CHUNK - 1:(ci + 1) * CHUNK, :]
            decay = jnp.exp2(jnp.transpose(jnp.broadcast_to(b_last, (GLA_DK, GLA_DK))))
            s_prev = jnp.concatenate([decay, decay], axis=1) * s_prev + kv_s[hh, ci]
        state[hh] = s_prev
    for hh, ci in pairs:
        rows = rows_of(ci)
        o = jnp.dot(sc_s[hh, rows, :], v_s[hh, rows, :], preferred_element_type=_F32)
        o = o + jnp.dot(qb_s[hh, rows, :], sprev_s[hh, ci], preferred_element_type=_F32)
        o_s[hh, rows, :] = o

    halves = [slice(i * (ts // 2), (i + 1) * (ts // 2)) for i in range(2)]

    def block_dots(lhs, w_blocks):
        return [jnp.dot(lhs, w[...], preferred_element_type=_F32) for w in w_blocks]

    y_conv = []
    for rows in halves:
        c = jnp.concatenate([conv_s[s, rows, :] for s in range(SLABS)], axis=1)
        mu = jnp.mean(c, axis=-1, keepdims=True)
        xc = c - mu
        ln = xc * lax.rsqrt(jnp.mean(xc * xc, axis=-1, keepdims=True) + EPS)
        ln = ln * ln_g_ref[...] + ln_b_ref[...]
        cz = jnp.concatenate([raw_s[_R_CZ, j, rows, :] for j in range(BLOCKS)], axis=1)
        a = _silu(ln) * cz
        y_conv.append(block_dots(a.astype(_BF16), w_conv_out))
    y_gla = []
    for rows in halves:
        gated = []
        for hh in range(GLA_HEADS):
            oh = o_s[hh, rows, :]
            oh = oh * lax.rsqrt(jnp.mean(oh * oh, axis=-1, keepdims=True) + EPS) * gla_g_ref[...]
            gated.append((oh * raw_s[_R_GR, hh, rows, :]).astype(_BF16))
        y_gla.append(block_dots(jnp.concatenate(gated, axis=1), w_gla_out))
    for i, rows in enumerate(halves):
        y = jnp.concatenate([raw_s[_R_MCONV, j, rows, :] * y_conv[i][j]
                             + raw_s[_R_MGLA, j, rows, :] * y_gla[i][j]
                             for j in range(BLOCKS)], axis=1)
        xn = x_ref[0, rows, :] + jnp.concatenate(block_dots(y.astype(_BF16), w_out), axis=1)
        if apply_final_norm:
            xn = xn * lax.rsqrt(jnp.mean(xn * xn, axis=-1, keepdims=True) + EPS) * final_g_ref[...]
        out_ref[0, rows, :] = xn


def _const_spec(shape):
    zeros = (0,) * len(shape)
    return pl.BlockSpec(shape, lambda b, s: zeros, pipeline_mode=pl.Buffered(1))


def _column_windows(w):
    k, n = w.shape
    return [pl.BlockSpec((k, MXU_COLS), lambda b, s, j=j: (0, j), pipeline_mode=pl.Buffered(1))
            for j in range(n // MXU_COLS)]


def _layer(x, norm_g, w_glr, tril, conv_w, conv_b, ln_g, ln_b, gate_w2, gate_b, gla_g, final_g,
           w_lo, w_hi, w_conv_out, w_gla_out, w_out, *, apply_final_norm):
    batch, seq, d = x.shape
    ts = SEQ_TILE
    assert d == D_MODEL and seq % ts == 0 and ts % CHUNK == 0
    assert HALO >= CONV_KERNEL - 1 and SUBLANES * sum(CONV_STRIDES) == ts
    assert all(s % 8 == 4 and s % CONV_GROUP == 0 for s in CONV_STRIDES)
    assert BLOCKS == GLA_HEADS and GLA_DV == MXU_COLS
    consts = (norm_g, w_glr, tril, conv_w, conv_b, ln_g, ln_b, gate_w2, gate_b, gla_g, final_g)
    blocked = (w_lo, w_hi, w_conv_out, w_gla_out, w_out)
    window_specs = [spec for w in blocked for spec in _column_windows(w)]
    window_args = [w for w in blocked for _ in range(w.shape[1] // MXU_COLS)]
    x_spec = pl.BlockSpec((1, ts, d), lambda b, s: (b, s, 0))
    head_rows = (GLA_HEADS, ts, GLA_DK)
    return pl.pallas_call(
        functools.partial(_layer_kernel, apply_final_norm=apply_final_norm),
        out_shape=jax.ShapeDtypeStruct(x.shape, x.dtype),
        grid=(batch, seq // ts),
        in_specs=[x_spec] + [_const_spec(c.shape) for c in consts] + window_specs,
        out_specs=x_spec,
        scratch_shapes=[
            pltpu.VMEM((ts, d), _BF16),
            pltpu.VMEM((SLABS, HALO + ts, LANES), _F32),
            pltpu.VMEM((SLABS, ts, LANES), _F32),
            pltpu.VMEM((len(_RAW), BLOCKS, ts, MXU_COLS), _F32),
            pltpu.VMEM((ts, GLA_KEY_DIM), _F32),
            pltpu.VMEM((ts, GLA_KEY_DIM), _F32),
            pltpu.VMEM(head_rows, _F32),
            pltpu.VMEM(head_rows, _BF16),
            pltpu.VMEM(head_rows, _BF16),
            pltpu.VMEM(head_rows, _BF16),
            pltpu.VMEM(head_rows, _BF16),
            pltpu.VMEM((GLA_HEADS, ts, GLA_DV), _BF16),
            pltpu.VMEM((GLA_HEADS, ts // CHUNK, GLA_DK, GLA_DV), _F32),
            pltpu.VMEM((GLA_HEADS, ts // CHUNK, GLA_DK, GLA_DV), _BF16),
            pltpu.VMEM((GLA_HEADS, ts, CHUNK), _BF16),
            pltpu.VMEM((GLA_HEADS, ts, GLA_DV), _F32),
            pltpu.VMEM((GLA_HEADS, GLA_DK, GLA_DV), _F32),
        ],
        compiler_params=pltpu.CompilerParams(
            dimension_semantics=("arbitrary", "arbitrary"),
            vmem_limit_bytes=VMEM_LIMIT_BYTES),
        name="conv_gla_layer",
    )(x, *consts, *window_args)


def kernel(x, norm_g, w_in, conv_w, conv_b, conv_ln_g, conv_ln_b, w_conv_out, gate_w2, gate_b,
           gla_norm_g, w_gla_out, w_out, final_g):
    depth = w_in.shape[0]
    t = jnp.arange(SEQ_TILE)
    tril = ((t[None, :] <= t[:, None]) & (t[None, :] // CHUNK == t[:, None] // CHUNK)).astype(_BF16)
    row = lambda v: v.reshape(1, -1)
    lr0 = 3 * D_MODEL + 2 * GLA_KEY_DIM + D_MODEL
    for l in range(depth):
        w = w_in[l]
        w_lo = w[:, :lr0].astype(_BF16)
        w_hi = w[:, lr0 + GATE_RANK:].astype(_BF16)
        w_glr = jnp.pad(w[:, lr0:lr0 + GATE_RANK], ((0, 0), (0, LANES - GATE_RANK))).astype(_BF16)
        gate_w2_p = jnp.pad(gate_w2[l], ((0, LANES - GATE_RANK), (0, 0))).astype(_BF16)
        conv_w_slabs = conv_w[l].reshape(CONV_KERNEL, SLABS, LANES).transpose(1, 0, 2)
        conv_b_slabs = conv_b[l].reshape(SLABS, 1, LANES)
        x = _layer(x, row(norm_g[l]), w_glr, tril, conv_w_slabs, conv_b_slabs,
                   row(conv_ln_g[l]), row(conv_ln_b[l]), gate_w2_p, row(gate_b[l]),
                   row(gla_norm_g[l]), row(final_g), w_lo, w_hi, w_conv_out[l].astype(_BF16),
                   w_gla_out[l].astype(_BF16), w_out[l].astype(_BF16),
                   apply_final_norm=(l == depth - 1))
    return x
```

```python
import functools

import jax
import jax.numpy as jnp
from jax import lax
from jax.experimental import pallas as pl
from jax.experimental.pallas import tpu as pltpu

D_MODEL = 1024
CONV_KERNEL = 31
GLA_HEADS = 4
GLA_KEY_DIM = D_MODEL // 2
GLA_DK = GLA_KEY_DIM // GLA_HEADS
GLA_DV = D_MODEL // GLA_HEADS
GATE_RANK = 16
GATE_TEMP = 16.0
CHUNK = 64
EPS = 1e-6

LANES = 128
SUBLANES = 8
MXU_COLS = 256
SLABS = D_MODEL // LANES
BLOCKS = D_MODEL // MXU_COLS
SLABS_PER_BLOCK = MXU_COLS // LANES
HALO = 32
SEQ_TILE = 512
CONV_STRIDES = (36, 28)
CONV_GROUP = 4
VMEM_LIMIT_BYTES = 60 * 1024 * 1024

_W_SIZES = (D_MODEL, D_MODEL, D_MODEL, GLA_KEY_DIM, GLA_KEY_DIM, D_MODEL, D_MODEL, D_MODEL, D_MODEL)
_W_BLK = tuple(sum(_W_SIZES[:i]) // MXU_COLS for i in range(len(_W_SIZES)))
(_C_VAL, _C_GATE, _C_Z, _Q, _K, _V, _G_R, _M_CONV, _M_GLA) = range(9)
_RAW = (_C_Z, _G_R, _M_CONV, _M_GLA)
(_R_CZ, _R_GR, _R_MCONV, _R_MGLA) = range(4)

_F32 = jnp.float32
_BF16 = jnp.bfloat16

LOG2E = 1.4426950408889634


def _sigmoid(x):
    return 1.0 / (1.0 + jnp.exp2(x * (-LOG2E)))


def _silu(x):
    return x * _sigmoid(x)


_RAW_ACT = (_silu, _silu, _sigmoid, _sigmoid)


def _split3(x):
    p0 = x.astype(_BF16)
    r = x - p0.astype(_F32)
    p1 = r.astype(_BF16)
    p2 = (r - p1.astype(_F32)).astype(_BF16)
    return p0, p1, p2


def _layer_kernel(x_ref, norm_g_ref, w_glr_ref, tril_ref, conv_w_ref, conv_b_ref,
                  ln_g_ref, ln_b_ref, gate_w2_ref, gate_b_ref, gla_g_ref, final_g_ref,
                  *refs, apply_final_norm):
    n_proj = sum(_W_SIZES) // MXU_COLS
    w_proj = refs[:n_proj]
    w_conv_out, w_gla_out, w_out = (refs[n_proj + i * BLOCKS:n_proj + (i + 1) * BLOCKS]
                                    for i in range(3))
    (out_ref, h_s, ubuf, conv_s, raw_s, q_s, k_s, b_s, qe_s, ke_s, kd_s, qb_s, v_s, kv_s, sprev_s,
     sc_s, o_s, state) = refs[n_proj + 3 * BLOCKS:]
    ts = x_ref.shape[1]
    n_chunks = ts // CHUNK

    @pl.when(pl.program_id(1) == 0)
    def _():
        ubuf[:, 0:HALO, :] = jnp.zeros((SLABS, HALO, LANES), _F32)
        state[...] = jnp.zeros(state.shape, _F32)

    x = x_ref[0]
    h = x * lax.rsqrt(jnp.mean(x * x, axis=-1, keepdims=True) + EPS) * norm_g_ref[...]
    h_s[...] = h.astype(_BF16)

    def proj_block(blk):
        return jnp.dot(h_s[...], w_proj[blk][...], preferred_element_type=_F32)

    def glu_block(j):
        u = proj_block(_W_BLK[_C_VAL] + j) * _sigmoid(proj_block(_W_BLK[_C_GATE] + j))
        for t in range(SLABS_PER_BLOCK):
            ubuf[j * SLABS_PER_BLOCK + t, HALO:HALO + ts, :] = u[:, t * LANES:(t + 1) * LANES]

    def raw_block(i):
        v_s[i] = proj_block(_W_BLK[_V] + i).astype(_BF16)
        for slot, idx in enumerate(_RAW):
            raw_s[slot, i] = _RAW_ACT[slot](proj_block(_W_BLK[idx] + i))

    def conv_block(i):
        for t in range(SLABS_PER_BLOCK):
            slab = i * SLABS_PER_BLOCK + t
            base = 0
            for stride in CONV_STRIDES:
                for r_lo in range(0, stride, CONV_GROUP):
                    accs = [jnp.broadcast_to(conv_b_ref[slab], (SUBLANES, LANES))] * CONV_GROUP
                    for k in range(CONV_KERNEL):
                        wk = conv_w_ref[slab, k:k + 1, :]
                        for g in range(CONV_GROUP):
                            start = HALO + base + r_lo + g - (CONV_KERNEL - 1) + k
                            win = ubuf[slab, pl.ds(start, SUBLANES, stride=stride), :]
                            accs[g] = accs[g] + wk * win
                    for g in range(CONV_GROUP):
                        conv_s[slab, pl.ds(base + r_lo + g, SUBLANES, stride=stride), :] = accs[g]
                base += SUBLANES * stride

    for j in range(BLOCKS):
        glu_block(j)
        conv_block(j)

    g_lr = jnp.dot(h_s[...], w_glr_ref[...], preferred_element_type=_F32)
    for j in range(GLA_KEY_DIM // MXU_COLS):
        cols = slice(j * MXU_COLS, (j + 1) * MXU_COLS)
        q_s[:, cols] = proj_block(_W_BLK[_Q] + j)
        k_s[:, cols] = proj_block(_W_BLK[_K] + j)
    z = jnp.dot(g_lr.astype(_BF16), gate_w2_ref[...],
                preferred_element_type=_F32) + gate_b_ref[...]
    log_a = jax.nn.log_sigmoid(z) * (1.0 / GATE_TEMP)
    log_a_pieces = _split3(log_a)

    for j in range(BLOCKS - 1):
        raw_block(j)

    b_all = sum(jnp.dot(tril_ref[...], p, preferred_element_type=_F32) for p in log_a_pieces)
    b_all = b_all * LOG2E
    for hh in range(GLA_HEADS):
        b_s[hh] = b_all[:, hh * GLA_DK:(hh + 1) * GLA_DK]

    for ci in range(n_chunks):
        rows = slice(ci * CHUNK, (ci + 1) * CHUNK)
        b = jnp.concatenate([b_s[hh, rows, :] for hh in range(GLA_HEADS)], axis=1)
        b_mid = b[CHUNK // 2 - 1:CHUNK // 2, :]
        b_last = b[CHUNK - 1:CHUNK, :]
        q = q_s[rows, :] * (GLA_DK ** -0.5)
        k = k_s[rows, :]
        qe = (q * jnp.exp2(b - b_mid)).astype(_BF16)
        ke = (k * jnp.exp2(b_mid - b)).astype(_BF16)
        kd = (k * jnp.exp2(b_last - b)).astype(_BF16)
        qb = (q * jnp.exp2(b)).astype(_BF16)
        for hh in range(GLA_HEADS):
            ks = slice(hh * GLA_DK, (hh + 1) * GLA_DK)
            qe_s[hh, rows, :] = qe[:, ks]
            ke_s[hh, rows, :] = ke[:, ks]
            kd_s[hh, rows, :] = kd[:, ks]
            qb_s[hh, rows, :] = qb[:, ks]

    raw_block(BLOCKS - 1)
    ubuf[:, 0:HALO, :] = ubuf[:, ts:ts + HALO, :]

    pairs = [(hh, ci) for hh in range(GLA_HEADS) for ci in range(n_chunks)]
    rows_of = lambda ci: slice(ci * CHUNK, (ci + 1) * CHUNK)
    crow = lax.broadcasted_iota(jnp.int32, (CHUNK, CHUNK), 0)
    ccol = lax.broadcasted_iota(jnp.int32, (CHUNK, CHUNK), 1)
    causal = ccol <= crow
    for hh, ci in pairs:
        rows = rows_of(ci)
        kv_s[hh, ci] = lax.dot_general(kd_s[hh, rows, :], v_s[hh, rows, :],
                                       (((0,), (0,)), ((), ())), preferred_element_type=_F32)
    for hh, ci in pairs:
        rows = rows_of(ci)
        scores = lax.dot_general(qe_s[hh, rows, :], ke_s[hh, rows, :],
                                 (((1,), (1,)), ((), ())), preferred_element_type=_F32)
        sc_s[hh, rows, :] = jnp.where(causal, scores, 0.0).astype(_BF16)
    for hh in range(GLA_HEADS):
        s_prev = state[hh]
        for ci in range(n_chunks):
            sprev_s[hh, ci] = s_prev.astype(_BF16)
            b_last = b_s[hh, ci * CHUNK + CHUNK - 1:(ci + 1) * CHUNK, :]
            decay = jnp.exp2(jnp.transpose(jnp.broadcast_to(b_last, (GLA_DK, GLA_DK))))
            s_prev = jnp.concatenate([decay, decay], axis=1) * s_prev + kv_s[hh, ci]
        state[hh] = s_prev
    for hh, ci in pairs:
        rows = rows_of(ci)
        o = jnp.dot(sc_s[hh, rows, :], v_s[hh, rows, :], preferred_element_type=_F32)
        o = o + jnp.dot(qb_s[hh, rows, :], sprev_s[hh, ci], preferred_element_type=_F32)
        o_s[hh, rows, :] = o

    halves = [slice(i * (ts // 2), (i + 1) * (ts // 2)) for i in range(2)]

    def block_dots(lhs, w_blocks):
        return [jnp.dot(lhs, w[...], preferred_element_type=_F32) for w in w_blocks]

    y_conv = []
    for rows in halves:
        c = jnp.concatenate([conv_s[s, rows, :] for s in range(SLABS)], axis=1)
        mu = jnp.mean(c, axis=-1, keepdims=True)
        xc = c - mu
        ln = xc * lax.rsqrt(jnp.mean(xc * xc, axis=-1, keepdims=True) + EPS)
        ln = ln * ln_g_ref[...] + ln_b_ref[...]
        cz = jnp.concatenate([raw_s[_R_CZ, j, rows, :] for j in range(BLOCKS)], axis=1)
        a = _silu(ln) * cz
        y_conv.append(block_dots(a.astype(_BF16), w_conv_out))
    y_gla = []
    for rows in halves:
        gated = []
        for hh in range(GLA_HEADS):
            oh = o_s[hh, rows, :]
            oh = oh * lax.rsqrt(jnp.mean(oh * oh, axis=-1, keepdims=True) + EPS) * gla_g_ref[...]
            gated.append((oh * raw_s[_R_GR, hh, rows, :]).astype(_BF16))
        y_gla.append(block_dots(jnp.concatenate(gated, axis=1), w_gla_out))
    for i, rows in enumerate(halves):
        y = jnp.concatenate([raw_s[_R_MCONV, j, rows, :] * y_conv[i][j]
                             + raw_s[_R_MGLA, j, rows, :] * y_gla[i][j]
                             for j in range(BLOCKS)], axis=1)
        xn = x_ref[0, rows, :] + jnp.concatenate(block_dots(y.astype(_BF16), w_out), axis=1)
        if apply_final_norm:
            xn = xn * lax.rsqrt(jnp.mean(xn * xn, axis=-1, keepdims=True) + EPS) * final_g_ref[...]
        out_ref[0, rows, :] = xn


def _const_spec(shape):
    zeros = (0,) * len(shape)
    return pl.BlockSpec(shape, lambda b, s: zeros, pipeline_mode=pl.Buffered(1))


def _column_windows(w):
    k, n = w.shape
    return [pl.BlockSpec((k, MXU_COLS), lambda b, s, j=j: (0, j), pipeline_mode=pl.Buffered(1))
            for j in range(n // MXU_COLS)]


def _layer(x, norm_g, w_glr, tril, conv_w, conv_b, ln_g, ln_b, gate_w2, gate_b, gla_g, final_g,
           w_lo, w_hi, w_conv_out, w_gla_out, w_out, *, apply_final_norm):
    batch, seq, d = x.shape
    ts = SEQ_TILE
    assert d == D_MODEL and seq % ts == 0 and ts % CHUNK == 0
    assert HALO >= CONV_KERNEL - 1 and SUBLANES * sum(CONV_STRIDES) == ts
    assert all(s % 8 == 4 and s % CONV_GROUP == 0 for s in CONV_STRIDES)
    assert BLOCKS == GLA_HEADS and GLA_DV == MXU_COLS
    consts = (norm_g, w_glr, tril, conv_w, conv_b, ln_g, ln_b, gate_w2, gate_b, gla_g, final_g)
    blocked = (w_lo, w_hi, w_conv_out, w_gla_out, w_out)
    window_specs = [spec for w in blocked for spec in _column_windows(w)]
    window_args = [w for w in blocked for _ in range(w.shape[1] // MXU_COLS)]
    x_spec = pl.BlockSpec((1, ts, d), lambda b, s: (b, s, 0))
    head_rows = (GLA_HEADS, ts, GLA_DK)
    return pl.pallas_call(
        functools.partial(_layer_kernel, apply_final_norm=apply_final_norm),
        out_shape=jax.ShapeDtypeStruct(x.shape, x.dtype),
        grid=(batch, seq // ts),
        in_specs=[x_spec] + [_const_spec(c.shape) for c in consts] + window_specs,
        out_specs=x_spec,
        scratch_shapes=[
            pltpu.VMEM((ts, d), _BF16),
            pltpu.VMEM((SLABS, HALO + ts, LANES), _F32),
            pltpu.VMEM((SLABS, ts, LANES), _F32),
            pltpu.VMEM((len(_RAW), BLOCKS, ts, MXU_COLS), _F32),
            pltpu.VMEM((ts, GLA_KEY_DIM), _F32),
            pltpu.VMEM((ts, GLA_KEY_DIM), _F32),
            pltpu.VMEM(head_rows, _F32),
            pltpu.VMEM(head_rows, _BF16),
            pltpu.VMEM(head_rows, _BF16),
            pltpu.VMEM(head_rows, _BF16),
            pltpu.VMEM(head_rows, _BF16),
            pltpu.VMEM((GLA_HEADS, ts, GLA_DV), _BF16),
            pltpu.VMEM((GLA_HEADS, ts // CHUNK, GLA_DK, GLA_DV), _F32),
            pltpu.VMEM((GLA_HEADS, ts // CHUNK, GLA_DK, GLA_DV), _BF16),
            pltpu.VMEM((GLA_HEADS, ts, CHUNK), _BF16),
            pltpu.VMEM((GLA_HEADS, ts, GLA_DV), _F32),
            pltpu.VMEM((GLA_HEADS, GLA_DK, GLA_DV), _F32),
        ],
        compiler_params=pltpu.CompilerParams(
            dimension_semantics=("arbitrary", "arbitrary"),
            vmem_limit_bytes=VMEM_LIMIT_BYTES),
        name="conv_gla_layer",
    )(x, *consts, *window_args)


def kernel(x, norm_g, w_in, conv_w, conv_b, conv_ln_g, conv_ln_b, w_conv_out, gate_w2, gate_b,
           gla_norm_g, w_gla_out, w_out, final_g):
    depth = w_in.shape[0]
    t = jnp.arange(SEQ_TILE)
    tril = ((t[None, :] <= t[:, None]) & (t[None, :] // CHUNK == t[:, None] // CHUNK)).astype(_BF16)
    row = lambda v: v.reshape(1, -1)
    lr0 = 3 * D_MODEL + 2 * GLA_KEY_DIM + D_MODEL
    for l in range(depth):
        w = w_in[l]
        w_lo = w[:, :lr0].astype(_BF16)
        w_hi = w[:, lr0 + GATE_RANK:].astype(_BF16)
        w_glr = jnp.pad(w[:, lr0:lr0 + GATE_RANK], ((0, 0), (0, LANES - GATE_RANK))).astype(_BF16)
        gate_w2_p = jnp.pad(gate_w2[l], ((0, LANES - GATE_RANK), (0, 0))).astype(_BF16)
        conv_w_slabs = conv_w[l].reshape(CONV_KERNEL, SLABS, LANES).transpose(1, 0, 2)
        conv_b_slabs = conv_b[l].reshape(SLABS, 1, LANES)
        x = _layer(x, row(norm_g[l]), w_glr, tril, conv_w_slabs, conv_b_slabs,
                   row(conv_ln_g[l]), row(conv_ln_b[l]), gate_w2_p, row(gate_b[l]),
                   row(gla_norm_g[l]), row(final_g), w_lo, w_hi, w_conv_out[l].astype(_BF16),
                   w_gla_out[l].astype(_BF16), w_out[l].astype(_BF16),
                   apply_final_norm=(l == depth - 1))
    return x
```
